```python
import math
import jax, jax.numpy as jnp
from jax import lax
import numpy as np

D_MODEL = 1024
BATCH = 4
SEQ = 8192
DEPTH = 2

HEAD_DIM = 64
N_HEADS = D_MODEL // HEAD_DIM
N_KV_HEADS = 4
GRID_W = 64
NA_ROWS_MAX = 8
NA_COLS = 16
WINDOW = 128
BLOCK = 128
N_BUCKETS = 32
MAX_EXACT = 8
MAX_DISTANCE = 128
D_FF = 4 * D_MODEL
N_MIXERS = 2
N_A_LAYERS = (DEPTH + 1) // 2
N_B_LAYERS = DEPTH // 2
EPS = 1e-6

kernel_name = "hybrid_natten_swa_sqrelu_encoder"


def rms_norm(x, g):
    xf = x.astype(jnp.float32)
    y = xf * lax.rsqrt(jnp.mean(xf * xf, axis=-1, keepdims=True) + EPS)
    return (y * g.astype(jnp.float32)).astype(x.dtype)


def t5_bucket(rel):
    half = N_BUCKETS // 2
    rel = jnp.asarray(rel, dtype=jnp.int32)
    base = jnp.where(rel > 0, half, 0)
    n = jnp.abs(rel)
    nf = jnp.maximum(n, 1).astype(jnp.float32)
    large = MAX_EXACT + (jnp.log(nf / MAX_EXACT) / math.log(MAX_DISTANCE / MAX_EXACT)
                         * (half - MAX_EXACT)).astype(jnp.int32)
    large = jnp.minimum(large, half - 1)
    return base + jnp.where(n < MAX_EXACT, n, large)


def neighbourhood_attention(h, w_qkv, rpb, w_o):
    B, S, D = h.shape
    rows = S // GRID_W
    kr = min(NA_ROWS_MAX, rows)
    qkv = jnp.einsum('bsd,de->bse', h, w_qkv).reshape(B, rows, GRID_W, 3, N_HEADS, HEAD_DIM)
    q = qkv[..., 0, :, :] * (HEAD_DIM ** -0.5)
    k = qkv[..., 1, :, :]
    v = qkv[..., 2, :, :]
    cols = np.arange(GRID_W)
    col_start = np.clip(cols - NA_COLS // 2, 0, GRID_W - NA_COLS)
    col_idx = col_start[:, None] + np.arange(NA_COLS)[None, :]
    dc = col_idx - cols[:, None] + (NA_COLS - 1)
    rpb_c = rpb[:, :, dc]

    def row_block(args):
        r, q_r = args
        rs = jnp.clip(r - kr // 2, 0, rows - kr)
        k_rows = lax.dynamic_slice_in_dim(k, rs, kr, axis=1)
        v_rows = lax.dynamic_slice_in_dim(v, rs, kr, axis=1)
        k_win = k_rows[:, :, col_idx]
        v_win = v_rows[:, :, col_idx]
        s = jnp.einsum('bchd,brckhd->bhcrk', q_r, k_win).astype(jnp.float32)
        dr = rs + jnp.arange(kr) - r + (NA_ROWS_MAX - 1)
        bias = jnp.transpose(rpb_c[:, dr], (0, 2, 1, 3))
        s = s + bias[None].astype(jnp.float32)
        p = jax.nn.softmax(s.reshape(B, N_HEADS, GRID_W, kr * NA_COLS), axis=-1)
        p = p.reshape(B, N_HEADS, GRID_W, kr, NA_COLS).astype(v.dtype)
        return jnp.einsum('bhcrk,brckhd->bchd', p, v_win)

    q_rows = jnp.moveaxis(q, 1, 0)
    o = lax.map(row_block, (jnp.arange(rows), q_rows))
    o = jnp.moveaxis(o, 0, 1).reshape(B, S, D)
    return jnp.einsum('bsd,de->bse', o, w_o)


def sliding_window_gqa(h, w_qkv, sinks, rel_table, w_o):
    B, S, D = h.shape
    nb = S // BLOCK
    G = N_HEADS // N_KV_HEADS
    kvd = N_KV_HEADS * HEAD_DIM
    qkv = jnp.einsum('bsd,de->bse', h, w_qkv)
    q = qkv[..., :D].reshape(B, nb, BLOCK, N_KV_HEADS, G, HEAD_DIM) * (HEAD_DIM ** -0.5)
    k = qkv[..., D:D + kvd].reshape(B, S, N_KV_HEADS, HEAD_DIM)
    v = qkv[..., D + kvd:].reshape(B, S, N_KV_HEADS, HEAD_DIM)

    def band(t):
        tp = jnp.pad(t, ((0, 0), (BLOCK, BLOCK), (0, 0), (0, 0)))
        tp = tp.reshape(B, nb + 2, BLOCK, N_KV_HEADS, HEAD_DIM)
        return jnp.concatenate([tp[:, :-2], tp[:, 1:-1], tp[:, 2:]], axis=2)

    kb, vb = band(k), band(v)
    s = jnp.einsum('bnqhgd,bnjhd->bhgnqj', q, kb).astype(jnp.float32)
    rel = (np.arange(3 * BLOCK) - BLOCK)[None, :] - np.arange(BLOCK)[:, None]
    key_pos = np.arange(nb)[:, None] * BLOCK + np.arange(3 * BLOCK)[None, :] - BLOCK
    mask = (np.abs(rel) <= WINDOW)[None] & ((key_pos >= 0) & (key_pos < S))[:, None, :]
    bias = jnp.transpose(rel_table[t5_bucket(rel)], (2, 0, 1)).astype(jnp.float32)
    bias = bias.reshape(N_KV_HEADS, G, 1, BLOCK, 3 * BLOCK)
    s = jnp.where(jnp.asarray(mask), s + bias, jnp.finfo(jnp.float32).min)
    sink = sinks.astype(jnp.float32).reshape(1, N_KV_HEADS, G, 1, 1, 1)
    m = jnp.maximum(jnp.max(s, axis=-1, keepdims=True), sink)
    p = jnp.exp(s - m)
    p = p / (jnp.sum(p, axis=-1, keepdims=True) + jnp.exp(sink - m))
    o = jnp.einsum('bhgnqj,bnjhd->bnqhgd', p.astype(v.dtype), vb).reshape(B, S, D)
    return jnp.einsum('bsd,de->bse', o, w_o)


def squared_relu_mlp(h, w_in, w_out):
    u = jnp.square(jax.nn.relu(jnp.einsum('bsd,df->bsf', h, w_in)))
    return jnp.einsum('bsf,fd->bsd', u, w_out)


def setup_inputs(seed: int = 0) -> dict:
    key = jax.random.key(seed)
    ks = jax.random.split(key, 14)
    f32 = jnp.float32
    D = D_MODEL
    kvd = N_KV_HEADS * HEAD_DIM
    res_scale = (2 * DEPTH) ** -0.5
    return {
        "x": jax.random.normal(ks[0], (BATCH, SEQ, D), f32),
        "mix_norm": 1.0 + 0.01 * jax.random.normal(ks[1], (DEPTH, D), f32),
        "mlp_norm": 1.0 + 0.01 * jax.random.normal(ks[2], (DEPTH, D), f32),
        "final_norm": 1.0 + 0.01 * jax.random.normal(ks[3], (D,), f32),
        "na_w_qkv": jax.random.normal(ks[4], (N_A_LAYERS, D, 3 * D), f32) * D ** -0.5,
        "na_rpb": 0.1 * jax.random.normal(ks[5], (N_A_LAYERS, N_HEADS, 2 * NA_ROWS_MAX - 1, 2 * NA_COLS - 1), f32),
        "na_w_o": jax.random.normal(ks[6], (N_A_LAYERS, D, D), f32) * D ** -0.5 * res_scale,
        "swa_w_qkv": jax.random.normal(ks[7], (N_B_LAYERS, D, D + 2 * kvd), f32) * D ** -0.5,
        "swa_sinks": 0.5 * jax.random.normal(ks[8], (N_B_LAYERS, N_HEADS), f32),
        "swa_w_o": jax.random.normal(ks[9], (N_B_LAYERS, D, D), f32) * D ** -0.5 * res_scale,
        "t5_rel_table": 0.1 * jax.random.normal(ks[10], (N_BUCKETS, N_HEADS), f32),
        "mlp_w_in": jax.random.normal(ks[11], (DEPTH, D, D_FF), f32) * D ** -0.5,
        "mlp_w_out": jax.random.normal(ks[12], (DEPTH, D_FF, D), f32) * D_FF ** -0.5 * res_scale,
    }


def reference(x, mix_norm, mlp_norm, final_norm, na_w_qkv, na_rpb, na_w_o,
              swa_w_qkv, swa_sinks, swa_w_o, t5_rel_table, mlp_w_in, mlp_w_out):
    for i in range(DEPTH):
        j = i // N_MIXERS
        h = rms_norm(x, mix_norm[i])
        if i % N_MIXERS == 0:
            x = x + neighbourhood_attention(h, na_w_qkv[j], na_rpb[j], na_w_o[j])
        else:
            x = x + sliding_window_gqa(h, swa_w_qkv[j], swa_sinks[j], t5_rel_table, swa_w_o[j])
        h = rms_norm(x, mlp_norm[i])
        x = x + squared_relu_mlp(h, mlp_w_in[i], mlp_w_out[i])
    return rms_norm(x, final_norm)
```

```python
import functools
import math

import jax
import jax.numpy as jnp
import numpy as np
from jax import lax
from jax.experimental import pallas as pl
from jax.experimental.pallas import tpu as pltpu

D_MODEL = 1024
HEAD_DIM = 64
N_HEADS = D_MODEL // HEAD_DIM
N_KV_HEADS = 4
GROUP = N_HEADS // N_KV_HEADS
KV_DIM = N_KV_HEADS * HEAD_DIM
GRID_W = 64
NA_ROWS = 8
NA_COLS = 16
WINDOW = 128
BLOCK = 128
N_BUCKETS = 32
MAX_EXACT = 8
MAX_DISTANCE = 128
D_FF = 4 * D_MODEL
EPS = 1e-6
NEG = -1e30

VMEM_LIMIT = 56 * 1024 * 1024

BF16 = jnp.bfloat16
F32 = jnp.float32


def _rms(x, g):
    ms = jnp.mean(x * x, axis=-1, keepdims=True)
    return x * lax.rsqrt(ms + EPS) * g


def _norm_matmul_kernel(x_ref, g_ref, w_ref, o_ref):
    h = _rms(x_ref[...], g_ref[...]).astype(BF16)
    o_ref[...] = jnp.dot(h, w_ref[...], preferred_element_type=F32).astype(o_ref.dtype)


def _norm_matmul(x, g, w, *, tm=512):
    n, d = x.shape
    e = w.shape[1]
    return pl.pallas_call(
        _norm_matmul_kernel,
        grid=(n // tm,),
        in_specs=[
            pl.BlockSpec((tm, d), lambda i: (i, 0)),
            pl.BlockSpec((1, d), lambda i: (0, 0)),
            pl.BlockSpec((d, e), lambda i: (0, 0)),
        ],
        out_specs=pl.BlockSpec((tm, e), lambda i: (i, 0)),
        out_shape=jax.ShapeDtypeStruct((n, e), BF16),
        compiler_params=pltpu.CompilerParams(
            dimension_semantics=("parallel",), vmem_limit_bytes=VMEM_LIMIT),
        name="norm_matmul",
    )(x, g.reshape(1, d), w)


NA_QROWS = 8
NA_HALO = 4
NA_KROWS = NA_QROWS + 2 * NA_HALO


def _na_kernel(q_ref, k0_ref, k1_ref, k2_ref, k3_ref, v0_ref, v1_ref, v2_ref, v3_ref,
               bias_ref, o_ref, k_sc, v_sc, *, n_rows):
    i = pl.program_id(1)
    hb = NA_HALO * GRID_W
    for t, (kr, vr) in enumerate(((k0_ref, v0_ref), (k1_ref, v1_ref),
                                   (k2_ref, v2_ref), (k3_ref, v3_ref))):
        k_sc[t * hb:(t + 1) * hb, :] = kr[0]
        v_sc[t * hb:(t + 1) * hb, :] = vr[0]

    lane = lax.broadcasted_iota(jnp.int32, (GRID_W, 2 * HEAD_DIM), 1)
    lo = lane < HEAD_DIM
    nk = NA_ROWS * GRID_W

    def row_body(rr, carry):
        r = i * NA_QROWS + rr
        rs = jnp.clip(r - NA_ROWS // 2, 0, n_rows - NA_ROWS)
        ls = rs - (i * NA_QROWS - NA_HALO)
        d0 = rs - r + (NA_ROWS - 1)
        qoff = pl.multiple_of(rr * GRID_W, GRID_W)
        koff = pl.multiple_of(ls * GRID_W, GRID_W)
        for hp in range(N_HEADS // 2):
            cs = slice(hp * 2 * HEAD_DIM, (hp + 1) * 2 * HEAD_DIM)
            qp = q_ref[0, pl.ds(qoff, GRID_W), cs] * jnp.asarray(HEAD_DIM ** -0.5, BF16)
            kp = k_sc[pl.ds(koff, nk), cs]
            vp = v_sc[pl.ds(koff, nk), cs]
            zero = jnp.zeros_like(qp)
            qq = jnp.concatenate([jnp.where(lo, qp, zero), jnp.where(lo, zero, qp)], axis=0)
            s = lax.dot_general(qq, kp, (((1,), (1,)), ((), ())),
                                preferred_element_type=F32)
            b = jnp.concatenate(
                [jnp.concatenate([bias_ref[2 * hp + e, d0 + 2 * jj] for jj in range(NA_ROWS // 2)],
                                 axis=1) for e in range(2)], axis=0)
            s = s + b
            m = jnp.max(s, axis=-1, keepdims=True)
            p = jnp.exp(s - m)
            l = jnp.sum(p, axis=-1, keepdims=True)
            o = jnp.dot(p.astype(BF16), vp, preferred_element_type=F32) / l
            o_ref[0, pl.ds(qoff, GRID_W), cs] = jnp.where(lo, o[:GRID_W], o[GRID_W:]).astype(o_ref.dtype)
        return carry

    lax.fori_loop(0, NA_QROWS, row_body, 0)


def _na_bias_table(rpb):
    cols = np.arange(GRID_W)
    col_start = np.clip(cols - NA_COLS // 2, 0, GRID_W - NA_COLS)
    kc = cols[None, :]
    inside = (kc >= col_start[:, None]) & (kc < col_start[:, None] + NA_COLS)
    idx = np.clip(kc - cols[:, None] + (NA_COLS - 1), 0, 2 * NA_COLS - 2)
    rc = jnp.where(jnp.asarray(inside)[None, None], rpb[:, :, idx].astype(F32), NEG)
    return jnp.concatenate([rc[:, :-1], rc[:, 1:]], axis=-1)


def _na_attention(qkv, bias_tbl):
    bsz, s, _ = qkv.shape
    n_rows = s // GRID_W
    nblk = n_rows // NA_QROWS
    qb = NA_QROWS * GRID_W
    hb = NA_HALO * GRID_W
    n_hblk = n_rows // NA_HALO
    per = NA_QROWS // NA_HALO

    def halo_spec(t, col):
        def imap(b, i):
            return (b, jnp.clip(i * per - 1 + t, 0, n_hblk - 1), col)
        return pl.BlockSpec((1, hb, D_MODEL), imap)

    in_specs = [pl.BlockSpec((1, qb, D_MODEL), lambda b, i: (b, i, 0))]
    in_specs += [halo_spec(t, 1) for t in range(4)]
    in_specs += [halo_spec(t, 2) for t in range(4)]
    in_specs += [pl.BlockSpec(bias_tbl.shape, lambda b, i: (0, 0, 0, 0))]
    return pl.pallas_call(
        functools.partial(_na_kernel, n_rows=n_rows),
        grid=(bsz, nblk),
        in_specs=in_specs,
        out_specs=pl.BlockSpec((1, qb, D_MODEL), lambda b, i: (b, i, 0)),
        out_shape=jax.ShapeDtypeStruct((bsz, s, D_MODEL), BF16),
        scratch_shapes=[pltpu.VMEM((NA_KROWS * GRID_W, D_MODEL), BF16),
                        pltpu.VMEM((NA_KROWS * GRID_W, D_MODEL), BF16)],
        compiler_params=pltpu.CompilerParams(
            dimension_semantics=("parallel", "arbitrary"), vmem_limit_bytes=VMEM_LIMIT),
        name="na_attention",
    )(qkv, *([qkv] * 8), bias_tbl)


def _swa_kernel(q_ref, kp_ref, kc_ref, kn_ref, vp_ref, vc_ref, vn_ref, bias_ref, sink_ref,
                o_ref, *, n_blocks):
    n = pl.program_id(1)
    kcol = lax.broadcasted_iota(jnp.int32, (1, 3 * BLOCK), 1)
    edge = jnp.where(((n == 0) & (kcol < BLOCK)) | ((n == n_blocks - 1) & (kcol >= 2 * BLOCK)),
                     F32(NEG), F32(0.0))
    for kvh in range(N_KV_HEADS):
        ks = slice(kvh * HEAD_DIM, (kvh + 1) * HEAD_DIM)
        k = jnp.concatenate([kp_ref[0, :, ks], kc_ref[0, :, ks], kn_ref[0, :, ks]], axis=0)
        v = jnp.concatenate([vp_ref[0, :, ks], vc_ref[0, :, ks], vn_ref[0, :, ks]], axis=0)
        for gi in range(GROUP):
            h = kvh * GROUP + gi
            hs = slice(h * HEAD_DIM, (h + 1) * HEAD_DIM)
            q = q_ref[0, :, hs] * jnp.asarray(HEAD_DIM ** -0.5, BF16)
            s = lax.dot_general(q, k, (((1,), (1,)), ((), ())),
                                preferred_element_type=F32)
            s = s + bias_ref[h] + edge
            sink = sink_ref[h]
            m = jnp.maximum(jnp.max(s, axis=-1, keepdims=True), sink)
            p = jnp.exp(s - m)
            l = jnp.sum(p, axis=-1, keepdims=True) + jnp.exp(sink - m)
            o = jnp.dot(p.astype(BF16), v, preferred_element_type=F32) / l
            o_ref[0, :, hs] = o.astype(o_ref.dtype)


def _t5_bucket(rel):
    half = N_BUCKETS // 2
    rel = jnp.asarray(rel, dtype=jnp.int32)
    base = jnp.where(rel > 0, half, 0)
    n = jnp.abs(rel)
    nf = jnp.maximum(n, 1).astype(jnp.float32)
    large = MAX_EXACT + (jnp.log(nf / MAX_EXACT) / math.log(MAX_DISTANCE / MAX_EXACT)
                         * (half - MAX_EXACT)).astype(jnp.int32)
    large = jnp.minimum(large, half - 1)
    return base + jnp.where(n < MAX_EXACT, n, large)


def _swa_bias_table(rel_table):
    rel = (np.arange(3 * BLOCK) - BLOCK)[None, :] - np.arange(BLOCK)[:, None]
    bias = jnp.transpose(rel_table[_t5_bucket(rel)], (2, 0, 1)).astype(F32)
    return jnp.where(jnp.asarray(np.abs(rel) <= WINDOW)[None], bias, NEG)


def _swa_attention(qkv, bias_tbl, sinks):
    bsz, s, _ = qkv.shape
    nb = s // BLOCK
    k_col = D_MODEL // KV_DIM
    v_col = k_col + 1

    def nbr_spec(t, col):
        def imap(b, n):
            return (b, jnp.clip(n - 1 + t, 0, nb - 1), col)
        return pl.BlockSpec((1, BLOCK, KV_DIM), imap)

    in_specs = [pl.BlockSpec((1, BLOCK, D_MODEL), lambda b, n: (b, n, 0))]
    in_specs += [nbr_spec(t, k_col) for t in range(3)]
    in_specs += [nbr_spec(t, v_col) for t in range(3)]
    in_specs += [pl.BlockSpec(bias_tbl.shape, lambda b, n: (0, 0, 0)),
                 pl.BlockSpec(memory_space=pltpu.SMEM)]
    return pl.pallas_call(
        functools.partial(_swa_kernel, n_blocks=nb),
        grid=(bsz, nb),
        in_specs=in_specs,
        out_specs=pl.BlockSpec((1, BLOCK, D_MODEL), lambda b, n: (b, n, 0)),
        out_shape=jax.ShapeDtypeStruct((bsz, s, D_MODEL), BF16),
        compiler_params=pltpu.CompilerParams(
            dimension_semantics=("parallel", "arbitrary"), vmem_limit_bytes=VMEM_LIMIT),
        name="swa_attention",
    )(qkv, *([qkv] * 6), bias_tbl, sinks.astype(F32))


def _proj_mlp_kernel(a_ref, x_ref, wo_ref, g_ref, win_ref, wout_ref, gf_ref, out_ref,
                     x1_sc, h_sc, acc_sc, *, final_norm):
    j = pl.program_id(1)

    @pl.when(j == 0)
    def _():
        x1 = x_ref[...] + jnp.dot(a_ref[...], wo_ref[...], preferred_element_type=F32)
        x1_sc[...] = x1
        h_sc[...] = _rms(x1, g_ref[...]).astype(BF16)
        acc_sc[...] = jnp.zeros_like(acc_sc)

    u = jnp.dot(h_sc[...], win_ref[...], preferred_element_type=F32)
    u = jnp.square(jnp.maximum(u, 0.0)).astype(BF16)
    acc_sc[...] += jnp.dot(u, wout_ref[...], preferred_element_type=F32)

    @pl.when(j == pl.num_programs(1) - 1)
    def _():
        x2 = x1_sc[...] + acc_sc[...]
        if final_norm:
            x2 = _rms(x2, gf_ref[...])
        out_ref[...] = x2


def _proj_mlp(a, x, w_o, g, w_in, w_out, g_final, *, final_norm, tm=512, tf=1024):
    n, d = x.shape
    f = w_in.shape[1]
    return pl.pallas_call(
        functools.partial(_proj_mlp_kernel, final_norm=final_norm),
        grid=(n // tm, f // tf),
        in_specs=[
            pl.BlockSpec((tm, d), lambda i, j: (i, 0)),
            pl.BlockSpec((tm, d), lambda i, j: (i, 0)),
            pl.BlockSpec((d, d), lambda i, j: (0, 0)),
            pl.BlockSpec((1, d), lambda i, j: (0, 0)),
            pl.BlockSpec((d, tf), lambda i, j: (0, j)),
            pl.BlockSpec((tf, d), lambda i, j: (j, 0)),
            pl.BlockSpec((1, d), lambda i, j: (0, 0)),
        ],
        out_specs=pl.BlockSpec((tm, d), lambda i, j: (i, 0)),
        out_shape=jax.ShapeDtypeStruct((n, d), F32),
        scratch_shapes=[pltpu.VMEM((tm, d), F32), pltpu.VMEM((tm, d), BF16),
                        pltpu.VMEM((tm, d), F32)],
        compiler_params=pltpu.CompilerParams(
            dimension_semantics=("parallel", "arbitrary"), vmem_limit_bytes=VMEM_LIMIT),
        name="proj_mlp",
    )(a, x, w_o, g.reshape(1, d), w_in, w_out, g_final.reshape(1, d))


def kernel(x, mix_norm, mlp_norm, final_norm, na_w_qkv, na_rpb, na_w_o, swa_w_qkv, swa_sinks,
           swa_w_o, t5_rel_table, mlp_w_in, mlp_w_out):
    bsz, s, d = x.shape
    n = bsz * s
    depth = mix_norm.shape[0]
    xf = x.reshape(n, d)
    for i in range(depth):
        j = i // 2
        if i % 2 == 0:
            qkv = _norm_matmul(xf, mix_norm[i], na_w_qkv[j].astype(BF16))
            a = _na_attention(qkv.reshape(bsz, s, -1), _na_bias_table(na_rpb[j]))
            w_o = na_w_o[j]
        else:
            qkv = _norm_matmul(xf, mix_norm[i], swa_w_qkv[j].astype(BF16))
            a = _swa_attention(qkv.reshape(bsz, s, -1), _swa_bias_table(t5_rel_table), swa_sinks[j])
            w_o = swa_w_o[j]
        xf = _proj_mlp(a.reshape(n, d), xf, w_o.astype(BF16), mlp_norm[i],
                       mlp_w_in[i].astype(BF16), mlp_w_out[i].astype(BF16), final_norm,
                       final_norm=(i == depth - 1))
    return xf.reshape(bsz, s, d)
```

```python
import functools
import math

import jax
import jax.numpy as jnp
import numpy as np
from jax import lax
from jax.experimental import pallas as pl
from jax.experimental.pallas import tpu as pltpu

D_MODEL = 1024
HEAD_DIM = 64
N_HEADS = D_MODEL // HEAD_DIM
N_KV_HEADS = 4
GROUP = N_HEADS // N_KV_HEADS
KV_DIM = N_KV_HEADS * HEAD_DIM
GRID_W = 64
NA_ROWS = 8
NA_COLS = 16
WINDOW = 128
BLOCK = 128
N_BUCKETS = 32
MAX_EXACT = 8
MAX_DISTANCE = 128
EPS = 1e-6
NEG = -1e30
LOG2E = math.log2(math.e)
Q_SCALE = HEAD_DIM ** -0.5 * LOG2E
PAIR = 2 * HEAD_DIM

VMEM_LIMIT = 56 * 1024 * 1024

BF16 = jnp.bfloat16
F32 = jnp.float32


def _rms(x, g):
    ms = jnp.mean(x * x, axis=-1, keepdims=True)
    return x * lax.rsqrt(ms + EPS) * g


def _toeplitz(vec, n_rows, n_cols):
    length = vec.shape[-1]
    lead = vec.shape[:-1]
    x = jnp.broadcast_to(vec[..., None, :], lead + (n_rows, length))
    x = x.reshape(lead + (n_rows * length,))[..., :n_rows * (length - 1)]
    return x.reshape(lead + (n_rows, length - 1))[..., :n_cols]


def _norm_matmul_kernel(x_ref, g_ref, w_ref, o_ref, *, n_q):
    h = _rms(x_ref[...], g_ref[...]).astype(BF16)
    acc = jnp.dot(h, w_ref[...], preferred_element_type=F32)
    o_ref[:, :n_q] = (acc[:, :n_q] * Q_SCALE).astype(o_ref.dtype)
    o_ref[:, n_q:] = acc[:, n_q:].astype(o_ref.dtype)


def _norm_matmul(x, g, w, *, n_q, tm=512):
    n, d = x.shape
    e = w.shape[1]
    return pl.pallas_call(
        functools.partial(_norm_matmul_kernel, n_q=n_q),
        grid=(n // tm,),
        in_specs=[
            pl.BlockSpec((tm, d), lambda i: (i, 0)),
            pl.BlockSpec((1, d), lambda i: (0, 0)),
            pl.BlockSpec((d, e), lambda i: (0, 0)),
        ],
        out_specs=pl.BlockSpec((tm, e), lambda i: (i, 0)),
        out_shape=jax.ShapeDtypeStruct((n, e), BF16),
        compiler_params=pltpu.CompilerParams(
            dimension_semantics=("parallel",), vmem_limit_bytes=VMEM_LIMIT),
        name="norm_matmul",
    )(x, g.reshape(1, d), w)


NA_QROWS = 8
NA_HALO = 4
NA_KROWS = NA_QROWS + 2 * NA_HALO


def _na_kernel(q_ref, k0_ref, k1_ref, k2_ref, k3_ref, v0_ref, v1_ref, v2_ref, v3_ref,
               bias_ref, o_ref, k_sc, v_sc, s_sc, *, n_rows):
    i = pl.program_id(1)
    hb = NA_HALO * GRID_W
    for t, (kr, vr) in enumerate(((k0_ref, v0_ref), (k1_ref, v1_ref),
                                   (k2_ref, v2_ref), (k3_ref, v3_ref))):
        k_sc[t * hb:(t + 1) * hb, :] = kr[0]
        v_sc[t * hb:(t + 1) * hb, :] = vr[0]

    lane = lax.broadcasted_iota(jnp.int32, (GRID_W, PAIR), 1)
    lo = lane < HEAD_DIM
    nk = NA_ROWS * GRID_W

    def row_body(rr, carry):
        r = i * NA_QROWS + rr
        rs = jnp.clip(r - NA_ROWS // 2, 0, n_rows - NA_ROWS)
        ls = rs - (i * NA_QROWS - NA_HALO)
        d0 = rs - r + (NA_ROWS - 1)
        qoff = pl.multiple_of(rr * GRID_W, GRID_W)
        koff = pl.multiple_of(ls * GRID_W, GRID_W)
        for hp in range(N_HEADS // 2):
            cs = slice(hp * PAIR, (hp + 1) * PAIR)
            qp = q_ref[0, pl.ds(qoff, GRID_W), cs]
            kp = k_sc[pl.ds(koff, nk), cs]
            zero = jnp.zeros_like(qp)
            qq = jnp.concatenate([jnp.where(lo, qp, zero), jnp.where(lo, zero, qp)], axis=0)
            s = lax.dot_general(qq, kp, (((1,), (1,)), ((), ())),
                                preferred_element_type=F32)
            for e in range(2):
                for jj in range(NA_ROWS // 2):
                    rows = slice(e * GRID_W, (e + 1) * GRID_W)
                    cols = slice(jj * 2 * GRID_W, (jj + 1) * 2 * GRID_W)
                    s_sc[hp, rows, cols] = s[rows, cols] + bias_ref[2 * hp + e, d0 + 2 * jj]
        for hp in range(N_HEADS // 2):
            cs = slice(hp * PAIR, (hp + 1) * PAIR)
            vp = v_sc[pl.ds(koff, nk), cs]
            m = jnp.max(s_sc[hp], axis=-1, keepdims=True)
            p = jnp.exp2(s_sc[hp] - m)
            l = jnp.sum(p, axis=-1, keepdims=True)
            o = jnp.dot(p.astype(BF16), vp, preferred_element_type=F32) / l
            o_ref[0, pl.ds(qoff, GRID_W), cs] = jnp.where(lo, o[:GRID_W], o[GRID_W:]).astype(o_ref.dtype)
        return carry

    lax.fori_loop(0, NA_QROWS, row_body, 0)


def _na_bias_table(rpb):
    cols = np.arange(GRID_W)
    col_start = np.clip(cols - NA_COLS // 2, 0, GRID_W - NA_COLS)
    kc = cols[None, :]
    inside = (kc >= col_start[:, None]) & (kc < col_start[:, None] + NA_COLS)
    n_rel = 2 * NA_COLS - 1
    left = GRID_W - NA_COLS
    padded = jnp.pad(rpb.astype(F32) * LOG2E, ((0, 0), (0, 0), (left, 2 * GRID_W - n_rel - left)))
    rc = _toeplitz(jnp.roll(padded, -(GRID_W - 1), axis=-1), GRID_W, GRID_W)
    rc = jnp.where(jnp.asarray(inside)[None, None], rc, NEG)
    return jnp.concatenate([rc[:, :-1], rc[:, 1:]], axis=-1)


def _na_attention(qkv, bias_tbl):
    bsz, s, _ = qkv.shape
    n_rows = s // GRID_W
    nblk = n_rows // NA_QROWS
    qb = NA_QROWS * GRID_W
    hb = NA_HALO * GRID_W
    n_hblk = n_rows // NA_HALO
    per = NA_QROWS // NA_HALO

    def halo_spec(t, col):
        def imap(b, i):
            return (b, jnp.clip(i * per - 1 + t, 0, n_hblk - 1), col)
        return pl.BlockSpec((1, hb, D_MODEL), imap)

    in_specs = [pl.BlockSpec((1, qb, D_MODEL), lambda b, i: (b, i, 0))]
    in_specs += [halo_spec(t, 1) for t in range(4)]
    in_specs += [halo_spec(t, 2) for t in range(4)]
    in_specs += [pl.BlockSpec(bias_tbl.shape, lambda b, i: (0, 0, 0, 0))]
    return pl.pallas_call(
        functools.partial(_na_kernel, n_rows=n_rows),
        grid=(bsz, nblk),
        in_specs=in_specs,
        out_specs=pl.BlockSpec((1, qb, D_MODEL), lambda b, i: (b, i, 0)),
        out_shape=jax.ShapeDtypeStruct((bsz, s, D_MODEL), BF16),
        scratch_shapes=[pltpu.VMEM((NA_KROWS * GRID_W, D_MODEL), BF16),
                        pltpu.VMEM((NA_KROWS * GRID_W, D_MODEL), BF16),
                        pltpu.VMEM((N_HEADS // 2, 2 * GRID_W, NA_ROWS * GRID_W), F32)],
        compiler_params=pltpu.CompilerParams(
            dimension_semantics=("parallel", "arbitrary"), vmem_limit_bytes=VMEM_LIMIT),
        name="na_attention",
    )(qkv, *([qkv] * 8), bias_tbl)


N_KV_PAIRS = N_KV_HEADS // 2


def _swa_head(gp, gi, e):
    return (2 * gp + e) * GROUP + gi


def _swa_kernel(q_ref, kp_ref, kc_ref, kn_ref, vp_ref, vc_ref, vn_ref, bias_ref, sink_ref,
                o_ref, s_sc):
    lane = lax.broadcasted_iota(jnp.int32, (BLOCK, PAIR), 1)
    lo = lane < HEAD_DIM
    for gp in range(N_KV_PAIRS):
        cs = slice(gp * PAIR, (gp + 1) * PAIR)
        kt = jnp.concatenate([kp_ref[0, :, cs], kc_ref[0, :, cs], kn_ref[0, :, cs]], axis=0)
        parts = []
        for gi in range(GROUP):
            t = gp * GROUP + gi
            qt = q_ref[0, :, t * PAIR:(t + 1) * PAIR]
            zero = jnp.zeros_like(qt)
            parts += [jnp.where(lo, qt, zero), jnp.where(lo, zero, qt)]
        qq = jnp.concatenate(parts, axis=0)
        s = lax.dot_general(qq, kt, (((1,), (1,)), ((), ())),
                            preferred_element_type=F32)
        for gi in range(GROUP):
            for e in range(2):
                rows = slice((2 * gi + e) * BLOCK, (2 * gi + e + 1) * BLOCK)
                s_sc[gp, rows, :] = s[rows] + bias_ref[0, _swa_head(gp, gi, e)]
    for gp in range(N_KV_PAIRS):
        cs = slice(gp * PAIR, (gp + 1) * PAIR)
        vt = jnp.concatenate([vp_ref[0, :, cs], vc_ref[0, :, cs], vn_ref[0, :, cs]], axis=0)
        for gi in range(GROUP):
            outs = []
            for e in range(2):
                rows = slice((2 * gi + e) * BLOCK, (2 * gi + e + 1) * BLOCK)
                sink = sink_ref[_swa_head(gp, gi, e)]
                m = jnp.maximum(jnp.max(s_sc[gp, rows, :], axis=-1, keepdims=True), sink)
                p = jnp.exp2(s_sc[gp, rows, :] - m)
                l = jnp.sum(p, axis=-1, keepdims=True) + jnp.exp2(sink - m)
                outs.append(jnp.dot(p.astype(BF16), vt, preferred_element_type=F32) / l)
            t = gp * GROUP + gi
            o_ref[0, :, t * PAIR:(t + 1) * PAIR] = jnp.where(lo, outs[0], outs[1]).astype(o_ref.dtype)


def _t5_bucket(rel):
    half = N_BUCKETS // 2
    rel = jnp.asarray(rel, dtype=jnp.int32)
    base = jnp.where(rel > 0, half, 0)
    n = jnp.abs(rel)
    nf = jnp.maximum(n, 1).astype(jnp.float32)
    large = MAX_EXACT + (jnp.log(nf / MAX_EXACT) / math.log(MAX_DISTANCE / MAX_EXACT)
                         * (half - MAX_EXACT)).astype(jnp.int32)
    large = jnp.minimum(large, half - 1)
    return base + jnp.where(n < MAX_EXACT, n, large)


def _swa_bias_table(rel_table):
    n_rel = 4 * BLOCK
    rel_vec = np.arange(n_rel) - (2 * BLOCK - 1)
    by_rel = jnp.transpose(rel_table[_t5_bucket(rel_vec)]).astype(F32) * LOG2E
    band = _toeplitz(jnp.roll(by_rel, -(BLOCK - 1), axis=-1), BLOCK, 3 * BLOCK)
    j = np.arange(3 * BLOCK)[None, :]
    rel = (j - BLOCK) - np.arange(BLOCK)[:, None]
    in_window = np.abs(rel) <= WINDOW
    keep = np.stack([in_window & (j >= BLOCK), in_window, in_window & (j < 2 * BLOCK)])
    return jnp.where(jnp.asarray(keep)[:, None], band[None], NEG)


def _swa_attention(qkv, bias_tbl, sinks):
    bsz, s, _ = qkv.shape
    nb = s // BLOCK
    k_col = D_MODEL // KV_DIM
    v_col = k_col + 1

    def nbr_spec(t, col):
        def imap(b, n):
            return (b, jnp.clip(n - 1 + t, 0, nb - 1), col)
        return pl.BlockSpec((1, BLOCK, KV_DIM), imap)

    def bias_imap(b, n):
        return (jnp.where(n == 0, 0, jnp.where(n == nb - 1, 2, 1)), 0, 0, 0)

    in_specs = [pl.BlockSpec((1, BLOCK, D_MODEL), lambda b, n: (b, n, 0))]
    in_specs += [nbr_spec(t, k_col) for t in range(3)]
    in_specs += [nbr_spec(t, v_col) for t in range(3)]
    in_specs += [pl.BlockSpec((1,) + bias_tbl.shape[1:], bias_imap),
                 pl.BlockSpec(memory_space=pltpu.SMEM)]
    return pl.pallas_call(
        _swa_kernel,
        grid=(bsz, nb),
        in_specs=in_specs,
        out_specs=pl.BlockSpec((1, BLOCK, D_MODEL), lambda b, n: (b, n, 0)),
        out_shape=jax.ShapeDtypeStruct((bsz, s, D_MODEL), BF16),
        scratch_shapes=[pltpu.VMEM((N_KV_PAIRS, 2 * GROUP * BLOCK, 3 * BLOCK), F32)],
        compiler_params=pltpu.CompilerParams(
            dimension_semantics=("parallel", "arbitrary"), vmem_limit_bytes=VMEM_LIMIT),
        name="swa_attention",
    )(qkv, *([qkv] * 6), bias_tbl, sinks.astype(F32) * LOG2E)


def _swa_tile_heads(w, axis):
    shape = w.shape
    split = shape[:axis] + (N_KV_PAIRS, 2, GROUP, HEAD_DIM) + shape[axis + 1:]
    return jnp.swapaxes(w.reshape(split), axis + 1, axis + 2).reshape(shape)


def _proj_mlp_kernel(a_ref, x_ref, wo_ref, g_ref, win_ref, wout_ref, gf_ref, out_ref,
                     x1_sc, h_sc, acc_sc, *, final_norm):
    j = pl.program_id(1)

    @pl.when(j == 0)
    def _():
        x1 = x_ref[...] + jnp.dot(a_ref[...], wo_ref[...], preferred_element_type=F32)
        x1_sc[...] = x1
        h_sc[...] = _rms(x1, g_ref[...]).astype(BF16)
        acc_sc[...] = jnp.zeros_like(acc_sc)

    u = jnp.dot(h_sc[...], win_ref[...], preferred_element_type=F32)
    u = jnp.square(jnp.maximum(u, 0.0)).astype(BF16)
    acc_sc[...] += jnp.dot(u, wout_ref[...], preferred_element_type=F32)

    @pl.when(j == pl.num_programs(1) - 1)
    def _():
        x2 = x1_sc[...] + acc_sc[...]
        if final_norm:
            x2 = _rms(x2, gf_ref[...])
        out_ref[...] = x2


def _proj_mlp(a, x, w_o, g, w_in, w_out, g_final, *, final_norm, tm=512, tf=1024):
    n, d = x.shape
    f = w_in.shape[1]
    return pl.pallas_call(
        functools.partial(_proj_mlp_kernel, final_norm=final_norm),
        grid=(n // tm, f // tf),
        in_specs=[
            pl.BlockSpec((tm, d), lambda i, j: (i, 0)),
            pl.BlockSpec((tm, d), lambda i, j: (i, 0)),
            pl.BlockSpec((d, d), lambda i, j: (0, 0)),
            pl.BlockSpec((1, d), lambda i, j: (0, 0)),
            pl.BlockSpec((d, tf), lambda i, j: (0, j)),
            pl.BlockSpec((tf, d), lambda i, j: (j, 0)),
            pl.BlockSpec((1, d), lambda i, j: (0, 0)),
        ],
        out_specs=pl.BlockSpec((tm, d), lambda i, j: (i, 0)),
        out_shape=jax.ShapeDtypeStruct((n, d), F32),
        scratch_shapes=[pltpu.VMEM((tm, d), F32), pltpu.VMEM((tm, d), BF16),
                        pltpu.VMEM((tm, d), F32)],
        compiler_params=pltpu.CompilerParams(
            dimension_semantics=("parallel", "arbitrary"), vmem_limit_bytes=VMEM_LIMIT),
        name="proj_mlp",
    )(a, x, w_o, g.reshape(1, d), w_in, w_out, g_final.reshape(1, d))


def kernel(x, mix_norm, mlp_norm, final_norm, na_w_qkv, na_rpb, na_w_o, swa_w_qkv, swa_sinks,
           swa_w_o, t5_rel_table, mlp_w_in, mlp_w_out):
    bsz, s, d = x.shape
    n = bsz * s
    depth = mix_norm.shape[0]
    xf = x.reshape(n, d)
    for i in range(depth):
        j = i // 2
        if i % 2 == 0:
            qkv = _norm_matmul(xf, mix_norm[i], na_w_qkv[j].astype(BF16), n_q=d)
            a = _na_attention(qkv.reshape(bsz, s, -1), _na_bias_table(na_rpb[j]))
            w_o = na_w_o[j]
        else:
            w_qkv = jnp.concatenate([_swa_tile_heads(swa_w_qkv[j][:, :d], 1), swa_w_qkv[j][:, d:]],
                                    axis=1)
            qkv = _norm_matmul(xf, mix_norm[i], w_qkv.astype(BF16), n_q=d)
            a = _swa_attention(qkv.reshape(bsz, s, -1), _swa_bias_table(t5_rel_table), swa_sinks[j])
            w_o = _swa_tile_heads(swa_w_o[j], 0)
        xf = _proj_mlp(a.reshape(n, d), xf, w_o.astype(BF16), mlp_norm[i],
                       mlp_w_in[i].astype(BF16), mlp_w_out[i].astype(BF16), final_norm,
                       final_norm=(i == depth - 1))
    return xf.reshape(bsz, s, d)
```
